```python
import jax, jax.numpy as jnp
from jax import lax
import numpy as np

D_MODEL = 1024
BATCH = 2
SEQ = 8192
DEPTH = 2

GRID_W = 64
CTX_LEN = 256
Q_BLOCK = 128
ROPE_THETA = 10000.0
NORM_EPS = 1e-6
N_MOD = 9
D_FF = 2816

SC_WIDTH = 512
SC_KERNEL = 3
MLA_HEADS = 8
MLA_NOPE = 64
MLA_ROPE = 32
MLA_V = 64
MLA_Q_RANK = 256
MLA_KV_RANK = 128
E_IN = 3 * SC_WIDTH + MLA_Q_RANK + MLA_KV_RANK + MLA_ROPE
E_MIX = SC_WIDTH + MLA_HEADS * MLA_V
GQA_HEADS = 8
GQA_KV_HEADS = 2
GQA_HEAD_DIM = 64
CONF_WIDTH = 512
CONF_KERNEL = 31
O_IN = (GQA_HEADS + 2 * GQA_KV_HEADS) * GQA_HEAD_DIM + 2 * CONF_WIDTH
O_MIX = GQA_HEADS * GQA_HEAD_DIM + CONF_WIDTH
N_EVEN = (DEPTH + 1) // 2
N_ODD = DEPTH // 2

kernel_name = 'hybrid_dit_shortconv_mla_gqa_conformer_macaron'


def rms_norm(x, g):
    xf = x.astype(jnp.float32)
    y = xf * lax.rsqrt(jnp.mean(xf * xf, axis=-1, keepdims=True) + NORM_EPS)
    return (y * g.astype(jnp.float32)).astype(x.dtype)


def layer_norm(x, g, b):
    xf = x.astype(jnp.float32)
    mu = jnp.mean(xf, axis=-1, keepdims=True)
    var = jnp.mean(jnp.square(xf - mu), axis=-1, keepdims=True)
    y = (xf - mu) * lax.rsqrt(var + NORM_EPS) * g.astype(jnp.float32) + b.astype(jnp.float32)
    return y.astype(x.dtype)


def modulation(cond, w, b):
    m = jax.nn.silu(cond) @ w + b
    return m.reshape(-1, N_MOD, 1, D_MODEL)


def adaln(t, g, mm, j):
    return rms_norm(t, g) * (1 + mm[:, 3 * j + 1]) + mm[:, 3 * j]


def swiglu(t, w_in, w_out):
    gt, up = jnp.split(t @ w_in, 2, axis=-1)
    return (jax.nn.silu(gt) * up) @ w_out


def depthwise_conv(x, w, b=None):
    k, ch = w.shape
    y = lax.conv_general_dilated(x, w[:, None, :].astype(x.dtype), window_strides=(1,),
                                 padding=[(k // 2, k // 2)],
                                 dimension_numbers=('NWC', 'WIO', 'NWC'),
                                 feature_group_count=ch)
    if b is not None:
        y = y + b
    return y


def axial_rope(n_rows, rot_dim):
    row = jnp.repeat(jnp.arange(n_rows, dtype=jnp.float32), GRID_W)
    col = jnp.tile(jnp.arange(GRID_W, dtype=jnp.float32), n_rows)
    nf = rot_dim // 4
    inv = ROPE_THETA ** (-jnp.arange(nf, dtype=jnp.float32) / nf)
    ang = jnp.concatenate([row[:, None] * inv, col[:, None] * inv], axis=-1)
    return jnp.cos(ang), jnp.sin(ang)


def apply_rope(x, cos, sin):
    xp = x.reshape(x.shape[:-1] + (-1, 2))
    x0, x1 = xp[..., 0], xp[..., 1]
    c = cos[:, None, :].astype(x.dtype)
    s = sin[:, None, :].astype(x.dtype)
    return jnp.stack([x0 * c - x1 * s, x0 * s + x1 * c], axis=-1).reshape(x.shape)


def blocked_attention(q, k, v, scale):
    b, n, hq, dh = q.shape
    hkv, dv = k.shape[2], v.shape[3]
    g = hq // hkv
    nb = n // Q_BLOCK
    qb = q.reshape(b, nb, Q_BLOCK, hkv, g, dh).transpose(1, 0, 2, 3, 4, 5)

    def one_block(qq):
        s = jnp.einsum('btkgd,bskd->bkgts', qq, k).astype(jnp.float32) * scale
        p = jax.nn.softmax(s, axis=-1).astype(v.dtype)
        return jnp.einsum('bkgts,bskd->btkgd', p, v)

    o = lax.map(one_block, qb)
    return o.transpose(1, 0, 2, 3, 4, 5).reshape(b, n, hq, dv)


def _mla_qkv(qa, kva, kr, q_norm, w_q_b, kv_norm, w_kv_b, rope):
    b, n, _ = qa.shape
    q = (rms_norm(qa, q_norm) @ w_q_b).reshape(b, n, MLA_HEADS, MLA_NOPE + MLA_ROPE)
    kv = (rms_norm(kva, kv_norm) @ w_kv_b).reshape(b, n, MLA_HEADS, MLA_NOPE + MLA_V)
    q_nope, q_rope = q[..., :MLA_NOPE], q[..., MLA_NOPE:]
    k_nope, v = kv[..., :MLA_NOPE], kv[..., MLA_NOPE:]
    kr = kr[:, :, None, :]
    if rope is not None:
        q_rope = apply_rope(q_rope, *rope)
        kr = apply_rope(kr, *rope)
    k = jnp.concatenate([k_nope, jnp.broadcast_to(kr, (b, n, MLA_HEADS, MLA_ROPE))], axis=-1)
    q = jnp.concatenate([q_nope, q_rope], axis=-1)
    return q, k, v


def even_mixer(h, hc, rope, w_in, conv_w, q_norm, w_q_b, kv_norm, w_kv_b, w_out, ctx_out):
    cuts = [SC_WIDTH, 2 * SC_WIDTH, 3 * SC_WIDTH, 3 * SC_WIDTH + MLA_Q_RANK,
            3 * SC_WIDTH + MLA_Q_RANK + MLA_KV_RANK]
    sb, sc, sx, qa, kva, kr = jnp.split(h @ w_in, cuts, axis=-1)
    sbc, scc, sxc, qac, kvac, krc = jnp.split(hc @ w_in, cuts, axis=-1)
    q, k, v = _mla_qkv(qa, kva, kr, q_norm, w_q_b, kv_norm, w_kv_b, rope)
    qc, kc, vc = _mla_qkv(qac, kvac, krc, q_norm, w_q_b, kv_norm, w_kv_b, None)
    scale = (MLA_NOPE + MLA_ROPE) ** -0.5
    b, n, _ = h.shape
    o = blocked_attention(q, jnp.concatenate([k, kc], axis=1), jnp.concatenate([v, vc], axis=1), scale)
    y_a = sb * depthwise_conv(sc * sx, conv_w)
    y = jnp.concatenate([y_a, o.reshape(b, n, MLA_HEADS * MLA_V)], axis=-1) @ w_out
    yc = None
    if ctx_out:
        oc = blocked_attention(qc, kc, vc, scale)
        y_ac = sbc * depthwise_conv(scc * sxc, conv_w)
        yc = jnp.concatenate([y_ac, oc.reshape(b, hc.shape[1], MLA_HEADS * MLA_V)], axis=-1) @ w_out
    return y, yc


def odd_mixer(h, hc, rope, w_in, q_norm, k_norm, conv_w, conv_b, ln_g, ln_b, w_out, ctx_out):
    qw = GQA_HEADS * GQA_HEAD_DIM
    kw = GQA_KV_HEADS * GQA_HEAD_DIM
    cuts = [qw, qw + kw, qw + 2 * kw]

    def split(t):
        b, n, _ = t.shape
        q, k, v, u = jnp.split(t @ w_in, cuts, axis=-1)
        q = rms_norm(q.reshape(b, n, GQA_HEADS, GQA_HEAD_DIM), q_norm)
        k = rms_norm(k.reshape(b, n, GQA_KV_HEADS, GQA_HEAD_DIM), k_norm)
        v = v.reshape(b, n, GQA_KV_HEADS, GQA_HEAD_DIM)
        return q, k, v, u

    def conformer_conv(u):
        a, gt = jnp.split(u, 2, axis=-1)
        z = depthwise_conv(a * jax.nn.sigmoid(gt), conv_w, conv_b)
        return jax.nn.silu(layer_norm(z, ln_g, ln_b))

    q, k, v, u = split(h)
    qc, kc, vc, uc = split(hc)
    q = apply_rope(q, *rope)
    k = apply_rope(k, *rope)
    scale = GQA_HEAD_DIM ** -0.5
    b, n, _ = h.shape
    o = blocked_attention(q, jnp.concatenate([k, kc], axis=1), jnp.concatenate([v, vc], axis=1), scale)
    y = jnp.concatenate([o.reshape(b, n, qw), conformer_conv(u)], axis=-1) @ w_out
    yc = None
    if ctx_out:
        oc = blocked_attention(qc, kc, vc, scale)
        yc = jnp.concatenate([oc.reshape(b, hc.shape[1], qw), conformer_conv(uc)], axis=-1) @ w_out
    return y, yc


def setup_inputs(seed: int = 0) -> dict:
    key = jax.random.key(seed)
    ks = iter(jax.random.split(key, 32))
    f32 = jnp.float32

    def nrm(shape, fan_in, gain=1.0):
        return gain * fan_in ** -0.5 * jax.random.normal(next(ks), shape, f32)

    def gains(shape):
        return 1.0 + 0.05 * jax.random.normal(next(ks), shape, f32)

    def bias(shape):
        return 0.02 * jax.random.normal(next(ks), shape, f32)

    return {
        'x': jax.random.normal(next(ks), (BATCH, SEQ, D_MODEL), f32),
        'c': jax.random.normal(next(ks), (BATCH, D_MODEL), f32),
        'ctx': jax.random.normal(next(ks), (BATCH, CTX_LEN, D_MODEL), f32),
        'c_ctx': jax.random.normal(next(ks), (D_MODEL,), f32),
        'w_mod': nrm((DEPTH, D_MODEL, N_MOD * D_MODEL), D_MODEL, 0.5),
        'b_mod': bias((DEPTH, N_MOD * D_MODEL)),
        'norm_g': gains((DEPTH, 3, D_MODEL)),
        'w_ffn_in': nrm((DEPTH, 2, D_MODEL, 2 * D_FF), D_MODEL),
        'w_ffn_out': nrm((DEPTH, 2, D_FF, D_MODEL), D_FF),
        'e_w_in': nrm((N_EVEN, D_MODEL, E_IN), D_MODEL),
        'e_conv_w': nrm((N_EVEN, SC_KERNEL, SC_WIDTH), SC_KERNEL),
        'e_q_norm': gains((N_EVEN, MLA_Q_RANK)),
        'e_w_q_b': nrm((N_EVEN, MLA_Q_RANK, MLA_HEADS * (MLA_NOPE + MLA_ROPE)), MLA_Q_RANK),
        'e_kv_norm': gains((N_EVEN, MLA_KV_RANK)),
        'e_w_kv_b': nrm((N_EVEN, MLA_KV_RANK, MLA_HEADS * (MLA_NOPE + MLA_V)), MLA_KV_RANK),
        'e_w_out': nrm((N_EVEN, E_MIX, D_MODEL), E_MIX),
        'o_w_in': nrm((N_ODD, D_MODEL, O_IN), D_MODEL),
        'o_q_norm': gains((N_ODD, GQA_HEAD_DIM)),
        'o_k_norm': gains((N_ODD, GQA_HEAD_DIM)),
        'o_conv_w': nrm((N_ODD, CONF_KERNEL, CONF_WIDTH), CONF_KERNEL),
        'o_conv_b': bias((N_ODD, CONF_WIDTH)),
        'o_ln_g': gains((N_ODD, CONF_WIDTH)),
        'o_ln_b': bias((N_ODD, CONF_WIDTH)),
        'o_w_out': nrm((N_ODD, O_MIX, D_MODEL), O_MIX),
        'final_g': gains((D_MODEL,)),
    }


def reference(x, c, ctx, c_ctx, w_mod, b_mod, norm_g, w_ffn_in, w_ffn_out,
              e_w_in, e_conv_w, e_q_norm, e_w_q_b, e_kv_norm, e_w_kv_b, e_w_out,
              o_w_in, o_q_norm, o_k_norm, o_conv_w, o_conv_b, o_ln_g, o_ln_b, o_w_out,
              final_g):
    ROWS = x.shape[1] // GRID_W
    rope_mla = axial_rope(ROWS, MLA_ROPE)
    rope_gqa = axial_rope(ROWS, GQA_HEAD_DIM)
    h, hc = x, ctx
    for l in range(DEPTH):
        last = l == DEPTH - 1
        m = modulation(c, w_mod[l], b_mod[l])
        mc = modulation(c_ctx, w_mod[l], b_mod[l])
        h = h + 0.5 * m[:, 2] * swiglu(adaln(h, norm_g[l, 0], m, 0), w_ffn_in[l, 0], w_ffn_out[l, 0])
        hc = hc + 0.5 * mc[:, 2] * swiglu(adaln(hc, norm_g[l, 0], mc, 0), w_ffn_in[l, 0], w_ffn_out[l, 0])
        a = adaln(h, norm_g[l, 1], m, 1)
        ac = adaln(hc, norm_g[l, 1], mc, 1)
        if l % 2 == 0:
            i = l // 2
            y, yc = even_mixer(a, ac, rope_mla, e_w_in[i], e_conv_w[i], e_q_norm[i], e_w_q_b[i],
                               e_kv_norm[i], e_w_kv_b[i], e_w_out[i], not last)
        else:
            i = l // 2
            y, yc = odd_mixer(a, ac, rope_gqa, o_w_in[i], o_q_norm[i], o_k_norm[i], o_conv_w[i],
                              o_conv_b[i], o_ln_g[i], o_ln_b[i], o_w_out[i], not last)
        h = h + m[:, 5] * y
        h = h + 0.5 * m[:, 8] * swiglu(adaln(h, norm_g[l, 2], m, 2), w_ffn_in[l, 1], w_ffn_out[l, 1])
        if not last:
            hc = hc + mc[:, 5] * yc
            hc = hc + 0.5 * mc[:, 8] * swiglu(adaln(hc, norm_g[l, 2], mc, 2), w_ffn_in[l, 1], w_ffn_out[l, 1])
    return rms_norm(h, final_g)
```

```python
import functools

import jax
import jax.numpy as jnp
from jax import lax
from jax.experimental import pallas as pl
from jax.experimental.pallas import tpu as pltpu

F32 = jnp.float32
BF16 = jnp.bfloat16

D_MODEL = 1024
DEPTH = 2
GRID_W = 64
ROPE_THETA = 10000.0
NORM_EPS = 1e-6
N_MOD = 9
D_FF = 2816
N_HEADS = 8
HEAD_V = 64
SC_WIDTH = 512
SC_KERNEL = 3
MLA_NOPE = 64
MLA_ROPE = 32
MLA_Q_RANK = 256
MLA_KV_RANK = 128
GQA_KV_HEADS = 2
GQA_HEAD_DIM = 64
CONF_WIDTH = 512
CONF_KERNEL = 31

LANES = 128
HALO = 16
VMEM_LIMIT = 60 * 1024 * 1024

TOKEN_TILE = 512
Q_TILE = 512
FF_CHUNK = 256
MOD_COLS = 1024

E_MIX_COLS = 2048
O_QKV_COLS = N_HEADS * LANES + GQA_KV_HEADS * LANES + GQA_KV_HEADS * HEAD_V
O_MIX_COLS = O_QKV_COLS + 2 * CONF_WIDTH


def _dot(a, b):
    return jnp.dot(a, b, preferred_element_type=F32)


def _resident(shape):
    zeros = (0,) * len(shape)
    return pl.BlockSpec(shape, lambda *_: zeros, pipeline_mode=pl.Buffered(1))


def _params(n_grid):
    return pltpu.CompilerParams(dimension_semantics=("arbitrary",) * n_grid,
                                vmem_limit_bytes=VMEM_LIMIT)


def _mod_kernel(ct_ref, w_ref, b_ref, o_ref):
    o_ref[...] = jnp.zeros(o_ref.shape, F32)
    for r in range(3):
        s = ct_ref[r]
        s = s * jax.nn.sigmoid(s)
        for c in range(MOD_COLS // LANES):
            cols = slice(c * LANES, (c + 1) * LANES)
            prod = (w_ref[0, :, cols] * s).reshape(D_MODEL // 8, 8, LANES)
            row = prod.sum(axis=0).sum(axis=0, keepdims=True)
            o_ref[0, r:r + 1, cols] = row + b_ref[0, :, cols]


def _modulation(cond_t, w_mod, b_mod):
    n_cols = N_MOD * D_MODEL
    return pl.pallas_call(
        _mod_kernel,
        grid=(DEPTH, n_cols // MOD_COLS),
        in_specs=[
            pl.BlockSpec((3, D_MODEL, LANES), lambda l, j: (0, 0, 0)),
            pl.BlockSpec((1, D_MODEL, MOD_COLS), lambda l, j: (l, 0, j)),
            pl.BlockSpec((1, 1, MOD_COLS), lambda l, j: (l, 0, j)),
        ],
        out_specs=pl.BlockSpec((1, 8, MOD_COLS), lambda l, j: (l, 0, j)),
        out_shape=jax.ShapeDtypeStruct((DEPTH, 8, n_cols), F32),
        compiler_params=_params(2),
        name="modulation",
    )(cond_t, w_mod, b_mod.reshape(DEPTH, 1, n_cols))


def _rms(x, width):
    return x * lax.rsqrt(jnp.sum(x * x, axis=-1, keepdims=True) * (1.0 / width) + NORM_EPS)


def _adaln(x, g, scale, shift):
    return _rms(x, D_MODEL) * (g * (1.0 + scale)) + shift


def _swiglu(a, w_in_ref, w_out_ref, act_ref):
    for c in range(D_FF // FF_CHUNK):
        gt = _dot(a, w_in_ref[:, c * FF_CHUNK:(c + 1) * FF_CHUNK])
        up = _dot(a, w_in_ref[:, D_FF + c * FF_CHUNK:D_FF + (c + 1) * FF_CHUNK])
        act_ref[:, c * FF_CHUNK:(c + 1) * FF_CHUNK] = (gt * jax.nn.sigmoid(gt) * up).astype(BF16)
    return _dot(act_ref[...], w_out_ref[...])


def _rope(x, ta, tb, tc):
    return x * ta + pltpu.roll(x, LANES - 1, 1) * tb + pltpu.roll(x, 1, 1) * tc


def _pre_common(h_ref, mod_ref, g_ref, w_in_ref, w_out_ref, h1_ref, act_ref):
    x = h_ref[0]
    m = mod_ref[0]
    a = _adaln(x, g_ref[0:1], m[1:2], m[0:1]).astype(BF16)
    h1 = x + (0.5 * m[2:3]) * _swiglu(a, w_in_ref, w_out_ref, act_ref)
    h1_ref[0] = h1
    return _adaln(h1, g_ref[1:2], m[4:5], m[3:4]).astype(BF16)


def _pre_even_kernel(h_ref, mod_ref, g_ref, w_in_ref, w_out_ref, w_mix_ref, qn_ref, kvn_ref,
                     w_q_ref, w_kv_ref, ta_ref, tb_ref, tc_ref,
                     h1_ref, qt_ref, k_ref, vt_ref, sb_ref, u_ref, act_ref):
    a1 = _pre_common(h_ref, mod_ref, g_ref, w_in_ref, w_out_ref, h1_ref, act_ref)
    ta, tb, tc = ta_ref[...], tb_ref[...], tc_ref[...]

    p1 = _dot(a1, w_mix_ref[:, 0:3 * SC_WIDTH])
    sb_ref[0] = p1[:, 0:SC_WIDTH].astype(BF16)
    u_ref[0] = (p1[:, SC_WIDTH:2 * SC_WIDTH] * p1[:, 2 * SC_WIDTH:3 * SC_WIDTH]).astype(BF16)

    p2 = _dot(a1, w_mix_ref[:, 3 * SC_WIDTH:E_MIX_COLS])
    qn = (_rms(p2[:, 0:MLA_Q_RANK], MLA_Q_RANK) * qn_ref[...]).astype(BF16)
    scale = (MLA_NOPE + MLA_ROPE) ** -0.5
    q = _dot(qn, w_q_ref[...]) * scale
    q = jnp.concatenate([_rope(q[:, h * LANES:(h + 1) * LANES], ta, tb, tc) for h in range(N_HEADS)], axis=1)
    qt_ref[0] = q.T.astype(BF16)

    kvn = _rms(p2[:, MLA_Q_RANK:MLA_Q_RANK + MLA_KV_RANK], MLA_KV_RANK) * kvn_ref[...]
    kr = _rope(p2[:, MLA_Q_RANK + MLA_KV_RANK:], ta, tb, tc)
    kv = _dot(jnp.concatenate([kvn, kr], axis=1).astype(BF16), w_kv_ref[...])
    k_ref[0] = kv[:, 0:N_HEADS * LANES].astype(BF16)
    vt_ref[0, 0] = kv[:, N_HEADS * LANES:].T.astype(BF16)


def _pre_odd_kernel(h_ref, mod_ref, g_ref, w_in_ref, w_out_ref, w_mix_ref, qg_ref, kg_ref,
                    ta_ref, tb_ref, tc_ref,
                    h1_ref, qt_ref, k_ref, vt_ref, u_ref, act_ref):
    a1 = _pre_common(h_ref, mod_ref, g_ref, w_in_ref, w_out_ref, h1_ref, act_ref)
    ta, tb, tc = ta_ref[...], tb_ref[...], tc_ref[...]

    p = _dot(a1, w_mix_ref[:, 0:O_QKV_COLS])
    scale = GQA_HEAD_DIM ** -0.5

    def head(slot, gain):
        x = p[:, slot * LANES:(slot + 1) * LANES]
        return _rope(_rms(x, GQA_HEAD_DIM) * gain, ta, tb, tc)

    q = jnp.concatenate([head(h, qg_ref[...]) * scale for h in range(N_HEADS)], axis=1)
    qt_ref[0] = q.T.astype(BF16)
    k = jnp.concatenate([head(N_HEADS + g, kg_ref[...]) for g in range(GQA_KV_HEADS)], axis=1)
    k_ref[0] = k.astype(BF16)
    vt_ref[0, 0] = p[:, (N_HEADS + GQA_KV_HEADS) * LANES:O_QKV_COLS].T.astype(BF16)

    p2 = _dot(a1, w_mix_ref[:, O_QKV_COLS:O_MIX_COLS])
    u_ref[0] = (p2[:, 0:CONF_WIDTH] * jax.nn.sigmoid(p2[:, CONF_WIDTH:])).astype(BF16)


def _pre_call(kind, h, mod, mod_row, g, w_in, w_out, mix_weights, tables, tile):
    b, n, _ = h.shape
    nt = n // tile
    n_kv = N_HEADS if kind == "even" else GQA_KV_HEADS
    tok = lambda width: pl.BlockSpec((1, tile, width), lambda bi, i: (bi, i, 0))
    in_specs = [
        tok(D_MODEL),
        pl.BlockSpec((1, N_MOD, D_MODEL), lambda bi, i: (mod_row(bi), 0, 0)),
        _resident(g.shape), _resident(w_in.shape), _resident(w_out.shape),
    ] + [_resident(w.shape) for w in mix_weights] + [
        pl.BlockSpec((tile, LANES), lambda bi, i: (i, 0)) for _ in tables
    ]
    out_shape = [
        jax.ShapeDtypeStruct((b, n, D_MODEL), F32),
        jax.ShapeDtypeStruct((b, N_HEADS * LANES, n), BF16),
        jax.ShapeDtypeStruct((b, n, n_kv * LANES), BF16),
        jax.ShapeDtypeStruct((b, nt, n_kv * HEAD_V, tile), BF16),
    ]
    out_specs = [
        tok(D_MODEL),
        pl.BlockSpec((1, N_HEADS * LANES, tile), lambda bi, i: (bi, 0, i)),
        tok(n_kv * LANES),
        pl.BlockSpec((1, 1, n_kv * HEAD_V, tile), lambda bi, i: (bi, i, 0, 0)),
    ]
    n_conv = 2 if kind == "even" else 1
    out_shape += [jax.ShapeDtypeStruct((b, n, SC_WIDTH), BF16)] * n_conv
    out_specs += [tok(SC_WIDTH)] * n_conv
    return pl.pallas_call(
        _pre_even_kernel if kind == "even" else _pre_odd_kernel,
        grid=(b, nt),
        in_specs=in_specs,
        out_specs=out_specs,
        out_shape=out_shape,
        scratch_shapes=[pltpu.VMEM((tile, D_FF), BF16)],
        compiler_params=_params(2),
        name=f"pre_{kind}_{n}",
    )(h, mod, g, w_in, w_out, *mix_weights, *tables)


def _attn_kernel(*refs, n_kv, heads_per_step, sources):
    qt_ref = refs[0]
    kv_refs = refs[1:1 + 2 * len(sources)]
    o_ref, acc_ref, m_ref, l_ref = refs[1 + 2 * len(sources):]
    group = N_HEADS // n_kv

    acc_ref[...] = jnp.zeros(acc_ref.shape, F32)
    l_ref[...] = jnp.zeros(l_ref.shape, F32)
    m_ref[...] = jnp.full(m_ref.shape, -jnp.inf, F32)

    def step(heads, k_ref, vt_ref, j, tile):
        tiles = {}
        for h in heads:
            g = h // group
            if g not in tiles:
                tiles[g] = (k_ref[0, pl.ds(j * tile, tile), g * LANES:(g + 1) * LANES],
                            vt_ref[0, j, g * HEAD_V:(g + 1) * HEAD_V, :])
            kt, vt = tiles[g]
            s = _dot(kt, qt_ref[0, h * LANES:(h + 1) * LANES, :])
            m_old = m_ref[h:h + 1, :]
            m_new = jnp.maximum(m_old, jnp.max(s, axis=0, keepdims=True))
            p = jnp.exp(s - m_new)
            alpha = jnp.exp(m_old - m_new)
            l_ref[h:h + 1, :] = alpha * l_ref[h:h + 1, :] + jnp.sum(p, axis=0, keepdims=True)
            rows = slice(h * HEAD_V, (h + 1) * HEAD_V)
            acc_ref[rows, :] = alpha * acc_ref[rows, :] + _dot(vt, p.astype(BF16))
            m_ref[h:h + 1, :] = m_new

    for h0 in range(0, N_HEADS, heads_per_step):
        heads = tuple(range(h0, h0 + heads_per_step))
        for si, (n_tiles, tile) in enumerate(sources):
            k_ref, vt_ref = kv_refs[2 * si], kv_refs[2 * si + 1]
            if n_tiles == 1:
                step(heads, k_ref, vt_ref, 0, tile)
            else:
                def body(j, carry, k_ref=k_ref, vt_ref=vt_ref, tile=tile, heads=heads):
                    step(heads, k_ref, vt_ref, j, tile)
                    return carry
                lax.fori_loop(0, n_tiles, body, 0)

    for h in range(N_HEADS):
        rows = slice(h * HEAD_V, (h + 1) * HEAD_V)
        acc_ref[rows, :] = acc_ref[rows, :] / l_ref[h:h + 1, :]
    o_ref[0] = acc_ref[...].T.astype(BF16)


def _attention(qt, kv_sources, n_kv, q_tile, heads_per_step, name):
    b, _, nq = qt.shape
    in_specs = [pl.BlockSpec((1, N_HEADS * LANES, q_tile), lambda bi, i: (bi, 0, i))]
    args = [qt]
    sources = []
    for k, vt in kv_sources:
        s_len, n_tiles, tile = k.shape[1], vt.shape[1], vt.shape[3]
        in_specs.append(pl.BlockSpec((1, s_len, n_kv * LANES), lambda bi, i: (bi, 0, 0),
                                     pipeline_mode=pl.Buffered(1)))
        in_specs.append(pl.BlockSpec((1, n_tiles, n_kv * HEAD_V, tile), lambda bi, i: (bi, 0, 0, 0),
                                     pipeline_mode=pl.Buffered(1)))
        args += [k, vt]
        sources.append((n_tiles, tile))
    kernel = functools.partial(_attn_kernel, n_kv=n_kv, heads_per_step=heads_per_step,
                               sources=tuple(sources))
    return pl.pallas_call(
        kernel,
        grid=(b, nq // q_tile),
        in_specs=in_specs,
        out_specs=pl.BlockSpec((1, q_tile, N_HEADS * HEAD_V), lambda bi, i: (bi, i, 0)),
        out_shape=jax.ShapeDtypeStruct((b, nq, N_HEADS * HEAD_V), BF16),
        scratch_shapes=[pltpu.VMEM((N_HEADS * HEAD_V, q_tile), F32),
                        pltpu.VMEM((N_HEADS, q_tile), F32),
                        pltpu.VMEM((N_HEADS, q_tile), F32)],
        compiler_params=_params(2),
        name=name,
    )(*args)


def _conv(u_ref, up_ref, un_ref, cw_ref, ext_ref, taps):
    i, nt = pl.program_id(1), pl.num_programs(1)
    tile = u_ref.shape[1]
    prev = up_ref[0].astype(F32)
    nxt = un_ref[0].astype(F32)
    ext_ref[0:HALO, :] = jnp.where(i > 0, prev, jnp.zeros_like(prev))
    ext_ref[HALO:HALO + tile, :] = u_ref[0].astype(F32)
    ext_ref[HALO + tile:, :] = jnp.where(i < nt - 1, nxt, jnp.zeros_like(nxt))
    acc = None
    for t in range(taps):
        start = HALO - taps // 2 + t
        term = cw_ref[t:t + 1, :] * ext_ref[start:start + tile, :]
        acc = term if acc is None else acc + term
    return acc


def _post_tail(h1_ref, y, mod_ref, g_ref, w_in_ref, w_out_ref, fg_ref, out_ref, act_ref, final):
    m = mod_ref[0]
    h2 = h1_ref[0] + m[5:6] * y
    a2 = _adaln(h2, g_ref[2:3], m[7:8], m[6:7]).astype(BF16)
    h3 = h2 + (0.5 * m[8:9]) * _swiglu(a2, w_in_ref, w_out_ref, act_ref)
    if final:
        h3 = _rms(h3, D_MODEL) * fg_ref[...]
    out_ref[0] = h3


def _post_even_kernel(h1_ref, o_ref, sb_ref, u_ref, up_ref, un_ref, mod_ref, g_ref, cw_ref,
                      w_mo_ref, w_in_ref, w_out_ref, fg_ref, out_ref, act_ref, ext_ref, *, final):
    conv = _conv(u_ref, up_ref, un_ref, cw_ref, ext_ref, SC_KERNEL)
    ya = (sb_ref[0].astype(F32) * conv).astype(BF16)
    y = _dot(ya, w_mo_ref[0:SC_WIDTH, :]) + _dot(o_ref[0], w_mo_ref[SC_WIDTH:, :])
    _post_tail(h1_ref, y, mod_ref, g_ref, w_in_ref, w_out_ref, fg_ref, out_ref, act_ref, final)


def _post_odd_kernel(h1_ref, o_ref, u_ref, up_ref, un_ref, mod_ref, g_ref, cw_ref, cb_ref, lg_ref, lb_ref,
                     w_mo_ref, w_in_ref, w_out_ref, fg_ref, out_ref, act_ref, ext_ref, *, final):
    z = _conv(u_ref, up_ref, un_ref, cw_ref, ext_ref, CONF_KERNEL) + cb_ref[...]
    mu = jnp.mean(z, axis=-1, keepdims=True)
    zc = z - mu
    var = jnp.mean(zc * zc, axis=-1, keepdims=True)
    zn = zc * lax.rsqrt(var + NORM_EPS) * lg_ref[...] + lb_ref[...]
    cf = (zn * jax.nn.sigmoid(zn)).astype(BF16)
    attn_cols = N_HEADS * HEAD_V
    y = _dot(o_ref[0], w_mo_ref[0:attn_cols, :]) + _dot(cf, w_mo_ref[attn_cols:, :])
    _post_tail(h1_ref, y, mod_ref, g_ref, w_in_ref, w_out_ref, fg_ref, out_ref, act_ref, final)


def _post_call(kind, h1, o, conv_inputs, mod, mod_row, g, conv_params, w_mo, w_in, w_out, final_g, tile, final):
    b, n, _ = h1.shape
    nt = n // tile
    halo_per_tile = tile // HALO
    n_halo = n // HALO
    tok = lambda width: pl.BlockSpec((1, tile, width), lambda bi, i: (bi, i, 0))
    u = conv_inputs[-1]
    in_specs = [tok(D_MODEL), tok(N_HEADS * HEAD_V)] + [tok(SC_WIDTH) for _ in conv_inputs] + [
        pl.BlockSpec((1, HALO, SC_WIDTH), lambda bi, i: (bi, jnp.maximum(i * halo_per_tile - 1, 0), 0)),
        pl.BlockSpec((1, HALO, SC_WIDTH), lambda bi, i: (bi, jnp.minimum((i + 1) * halo_per_tile, n_halo - 1), 0)),
        pl.BlockSpec((1, N_MOD, D_MODEL), lambda bi, i: (mod_row(bi), 0, 0)),
        _resident(g.shape),
    ] + [_resident(c.shape) for c in conv_params] + [
        _resident(w_mo.shape), _resident(w_in.shape), _resident(w_out.shape), _resident(final_g.shape),
    ]
    kernel = functools.partial(_post_even_kernel if kind == "even" else _post_odd_kernel, final=final)
    return pl.pallas_call(
        kernel,
        grid=(b, nt),
        in_specs=in_specs,
        out_specs=tok(D_MODEL),
        out_shape=jax.ShapeDtypeStruct((b, n, D_MODEL), F32),
        scratch_shapes=[pltpu.VMEM((tile, D_FF), BF16),
                        pltpu.VMEM((tile + 2 * HALO, SC_WIDTH), F32)],
        compiler_params=_params(2),
        name=f"post_{kind}_{n}",
    )(h1, o, *conv_inputs, u, u, mod, g, *conv_params, w_mo, w_in, w_out, final_g)


def _rope_tables(n_rows, rot_dim, lane_offset):
    row = jnp.repeat(jnp.arange(n_rows, dtype=F32), GRID_W)
    col = jnp.tile(jnp.arange(GRID_W, dtype=F32), n_rows)
    nf = rot_dim // 4
    inv = ROPE_THETA ** (-jnp.arange(nf, dtype=F32) / nf)
    ang = jnp.concatenate([row[:, None] * inv, col[:, None] * inv], axis=-1)
    cos = jnp.repeat(jnp.cos(ang), 2, axis=-1)
    sin = jnp.repeat(jnp.sin(ang), 2, axis=-1)
    even = (jnp.arange(rot_dim) % 2 == 0)
    n = cos.shape[0]
    pad = lambda t, fill: jnp.concatenate(
        [jnp.full((n, lane_offset), fill, F32), t, jnp.full((n, LANES - lane_offset - rot_dim), fill, F32)], axis=1)
    return (pad(cos, 1.0), pad(jnp.where(even, -sin, 0.0), 0.0), pad(jnp.where(even, 0.0, sin), 0.0))


def _identity_tables(n):
    return (jnp.ones((n, LANES), F32), jnp.zeros((n, LANES), F32), jnp.zeros((n, LANES), F32))


def _pad_heads(w, n_heads, width):
    k = w.shape[0]
    w = w.reshape(k, n_heads, width)
    return jnp.pad(w, ((0, 0), (0, 0), (0, LANES - width))).reshape(k, n_heads * LANES)


def _even_weights(w_in, q_norm, w_q_b, kv_norm, w_kv_b):
    cut = 3 * SC_WIDTH + MLA_Q_RANK + MLA_KV_RANK
    kr_slot = jnp.pad(w_in[:, cut:], ((0, 0), (MLA_NOPE, LANES - MLA_NOPE - MLA_ROPE)))
    w_mix = jnp.concatenate([w_in[:, :cut], kr_slot], axis=1).astype(BF16)
    w_q = _pad_heads(w_q_b, N_HEADS, MLA_NOPE + MLA_ROPE).astype(BF16)
    kvb = w_kv_b.reshape(MLA_KV_RANK, N_HEADS, MLA_NOPE + HEAD_V)
    w_k = _pad_heads(kvb[:, :, :MLA_NOPE].reshape(MLA_KV_RANK, -1), N_HEADS, MLA_NOPE)
    w_v = kvb[:, :, MLA_NOPE:].reshape(MLA_KV_RANK, N_HEADS * HEAD_V)
    place = jnp.zeros((LANES, N_HEADS, LANES), F32)
    idx = jnp.arange(MLA_NOPE, MLA_NOPE + MLA_ROPE)
    place = place.at[idx, :, idx].set(1.0).reshape(LANES, N_HEADS * LANES)
    w_kv = jnp.concatenate([
        jnp.concatenate([w_k, w_v], axis=1),
        jnp.concatenate([place, jnp.zeros((LANES, N_HEADS * HEAD_V), F32)], axis=1)], axis=0).astype(BF16)
    return [w_mix, q_norm.reshape(1, -1), kv_norm.reshape(1, -1), w_q, w_kv]


def _odd_weights(w_in, q_norm, k_norm):
    qw = N_HEADS * GQA_HEAD_DIM
    kw = GQA_KV_HEADS * GQA_HEAD_DIM
    w_mix = jnp.concatenate([
        _pad_heads(w_in[:, :qw], N_HEADS, GQA_HEAD_DIM),
        _pad_heads(w_in[:, qw:qw + kw], GQA_KV_HEADS, GQA_HEAD_DIM),
        w_in[:, qw + kw:]], axis=1).astype(BF16)
    pad = lambda g: jnp.pad(g, (0, LANES - GQA_HEAD_DIM)).reshape(1, LANES)
    return [w_mix, pad(q_norm), pad(k_norm)]


def kernel(x, c, ctx, c_ctx, w_mod, b_mod, norm_g, w_ffn_in, w_ffn_out, e_w_in, e_conv_w, e_q_norm, e_w_q_b, e_kv_norm, e_w_kv_b, e_w_out, o_w_in, o_q_norm, o_k_norm, o_conv_w, o_conv_b, o_ln_g, o_ln_b, o_w_out, final_g):
    batch, seq, _ = x.shape
    ctx_len = ctx.shape[1]
    assert seq % TOKEN_TILE == 0 and seq % Q_TILE == 0 and ctx_len % HALO == 0 and ctx_len % LANES == 0

    cond = jnp.concatenate([c, c_ctx[None, :]], axis=0)
    cond_t = jnp.broadcast_to(cond[:, :, None], (batch + 1, D_MODEL, LANES))
    mod_all = _modulation(cond_t, w_mod, b_mod)[:, :batch + 1].reshape(DEPTH, batch + 1, N_MOD, D_MODEL)

    w_ffn_in = w_ffn_in.astype(BF16)
    w_ffn_out = w_ffn_out.astype(BF16)
    final_g2 = final_g.reshape(1, D_MODEL)
    lat_row = lambda bi: bi
    ctx_row = lambda bi: batch

    h, hc = x, ctx
    for l in range(DEPTH):
        last = l == DEPTH - 1
        i = l // 2
        mod = mod_all[l]
        g = norm_g[l]
        if l % 2 == 0:
            kind, n_kv, heads_per_step = "even", N_HEADS, 2
            mix_w = _even_weights(e_w_in[i], e_q_norm[i], e_w_q_b[i], e_kv_norm[i], e_w_kv_b[i])
            tables = _rope_tables(seq // GRID_W, MLA_ROPE, MLA_NOPE)
            conv_params = [e_conv_w[i]]
            w_mo = e_w_out[i].astype(BF16)
        else:
            kind, n_kv, heads_per_step = "odd", GQA_KV_HEADS, N_HEADS // GQA_KV_HEADS
            mix_w = _odd_weights(o_w_in[i], o_q_norm[i], o_k_norm[i])
            tables = _rope_tables(seq // GRID_W, GQA_HEAD_DIM, 0)
            conv_params = [o_conv_w[i], o_conv_b[i].reshape(1, -1), o_ln_g[i].reshape(1, -1), o_ln_b[i].reshape(1, -1)]
            w_mo = o_w_out[i].astype(BF16)

        lat = _pre_call(kind, h, mod, lat_row, g, w_ffn_in[l, 0], w_ffn_out[l, 0], mix_w, tables, TOKEN_TILE)
        cx = _pre_call(kind, hc, mod, ctx_row, g, w_ffn_in[l, 0], w_ffn_out[l, 0], mix_w,
                       _identity_tables(ctx_len), ctx_len)
        h1, qt, k, vt = lat[:4]
        hc1, qtc, kc, vtc = cx[:4]

        o = _attention(qt, [(k, vt), (kc, vtc)], n_kv, Q_TILE, heads_per_step, f"attn_{kind}")
        h = _post_call(kind, h1, o, lat[4:], mod, lat_row, g, conv_params, w_mo,
                       w_ffn_in[l, 1], w_ffn_out[l, 1], final_g2, TOKEN_TILE, final=last)
        if not last:
            oc = _attention(qtc, [(kc, vtc)], n_kv, ctx_len, heads_per_step, f"attn_{kind}_ctx")
            hc = _post_call(kind, hc1, oc, cx[4:], mod, ctx_row, g, conv_params, w_mo,
                            w_ffn_in[l, 1], w_ffn_out[l, 1], final_g2, ctx_len, final=False)
    return h
```

```python
import functools

import jax
import jax.numpy as jnp
from jax import lax
from jax.experimental import pallas as pl
from jax.experimental.pallas import tpu as pltpu

F32 = jnp.float32
BF16 = jnp.bfloat16

D_MODEL = 1024
DEPTH = 2
GRID_W = 64
ROPE_THETA = 10000.0
NORM_EPS = 1e-6
N_MOD = 9
D_FF = 2816
N_HEADS = 8
HEAD_V = 64
SC_WIDTH = 512
SC_KERNEL = 3
MLA_NOPE = 64
MLA_ROPE = 32
MLA_Q_RANK = 256
MLA_KV_RANK = 128
GQA_KV_HEADS = 2
GQA_HEAD_DIM = 64
CONF_WIDTH = 512
CONF_KERNEL = 31

LANES = 128
HALO = 16
VMEM_LIMIT = 60 * 1024 * 1024

TOKEN_TILE = 512
Q_TILE = 512
LOG2_E = 1.4426950408889634
FF_CHUNK = 256
MOD_COLS = 1024

E_MIX_COLS = 2048
O_QKV_COLS = N_HEADS * LANES + GQA_KV_HEADS * LANES + GQA_KV_HEADS * HEAD_V
O_MIX_COLS = O_QKV_COLS + 2 * CONF_WIDTH


def _dot(a, b):
    return jnp.dot(a, b, preferred_element_type=F32)


def _resident(shape):
    zeros = (0,) * len(shape)
    return pl.BlockSpec(shape, lambda *_: zeros, pipeline_mode=pl.Buffered(1))


def _params(n_grid):
    return pltpu.CompilerParams(dimension_semantics=("arbitrary",) * n_grid,
                                vmem_limit_bytes=VMEM_LIMIT)


def _mod_kernel(ct_ref, w_ref, b_ref, o_ref):
    o_ref[...] = jnp.zeros(o_ref.shape, F32)
    for r in range(3):
        s = ct_ref[r]
        s = s * jax.nn.sigmoid(s)
        for c in range(MOD_COLS // LANES):
            cols = slice(c * LANES, (c + 1) * LANES)
            prod = (w_ref[0, :, cols] * s).reshape(D_MODEL // 8, 8, LANES)
            row = prod.sum(axis=0).sum(axis=0, keepdims=True)
            o_ref[0, r:r + 1, cols] = row + b_ref[0, :, cols]


def _modulation(cond_t, w_mod, b_mod):
    n_cols = N_MOD * D_MODEL
    return pl.pallas_call(
        _mod_kernel,
        grid=(DEPTH, n_cols // MOD_COLS),
        in_specs=[
            pl.BlockSpec((3, D_MODEL, LANES), lambda l, j: (0, 0, 0)),
            pl.BlockSpec((1, D_MODEL, MOD_COLS), lambda l, j: (l, 0, j)),
            pl.BlockSpec((1, 1, MOD_COLS), lambda l, j: (l, 0, j)),
        ],
        out_specs=pl.BlockSpec((1, 8, MOD_COLS), lambda l, j: (l, 0, j)),
        out_shape=jax.ShapeDtypeStruct((DEPTH, 8, n_cols), F32),
        compiler_params=_params(2),
        name="modulation",
    )(cond_t, w_mod, b_mod.reshape(DEPTH, 1, n_cols))


def _rms(x, width):
    return x * lax.rsqrt(jnp.sum(x * x, axis=-1, keepdims=True) * (1.0 / width) + NORM_EPS)


def _adaln(x, g, scale, shift):
    return _rms(x, D_MODEL) * (g * (1.0 + scale)) + shift


def _swiglu(a, w_in_ref, w_out_ref, act_ref):
    for c in range(D_FF // FF_CHUNK):
        gt = _dot(a, w_in_ref[:, c * FF_CHUNK:(c + 1) * FF_CHUNK])
        up = _dot(a, w_in_ref[:, D_FF + c * FF_CHUNK:D_FF + (c + 1) * FF_CHUNK])
        act_ref[:, c * FF_CHUNK:(c + 1) * FF_CHUNK] = (gt * jax.nn.sigmoid(gt) * up).astype(BF16)
    return _dot(act_ref[...], w_out_ref[...])


def _rope(x, ta, tb, tc):
    return x * ta + pltpu.roll(x, LANES - 1, 1) * tb + pltpu.roll(x, 1, 1) * tc


def _pre_common(h_ref, mod_ref, g_ref, w_in_ref, w_out_ref, h1_ref, act_ref):
    x = h_ref[0]
    m = mod_ref[0]
    a = _adaln(x, g_ref[0:1], m[1:2], m[0:1]).astype(BF16)
    h1 = x + (0.5 * m[2:3]) * _swiglu(a, w_in_ref, w_out_ref, act_ref)
    h1_ref[0] = h1
    return _adaln(h1, g_ref[1:2], m[4:5], m[3:4]).astype(BF16)


def _pre_even_kernel(h_ref, mod_ref, g_ref, w_in_ref, w_out_ref, w_mix_ref, qn_ref, kvn_ref,
                     w_q_ref, w_kv_ref, ta_ref, tb_ref, tc_ref,
                     h1_ref, qt_ref, k_ref, vt_ref, sb_ref, u_ref, act_ref):
    a1 = _pre_common(h_ref, mod_ref, g_ref, w_in_ref, w_out_ref, h1_ref, act_ref)
    ta, tb, tc = ta_ref[...], tb_ref[...], tc_ref[...]

    p1 = _dot(a1, w_mix_ref[:, 0:3 * SC_WIDTH])
    sb_ref[0] = p1[:, 0:SC_WIDTH].astype(BF16)
    u_ref[0] = (p1[:, SC_WIDTH:2 * SC_WIDTH] * p1[:, 2 * SC_WIDTH:3 * SC_WIDTH]).astype(BF16)

    p2 = _dot(a1, w_mix_ref[:, 3 * SC_WIDTH:E_MIX_COLS])
    qn = (_rms(p2[:, 0:MLA_Q_RANK], MLA_Q_RANK) * qn_ref[...]).astype(BF16)
    scale = (MLA_NOPE + MLA_ROPE) ** -0.5 * LOG2_E
    q = _dot(qn, w_q_ref[...]) * scale
    q = jnp.concatenate([_rope(q[:, h * LANES:(h + 1) * LANES], ta, tb, tc) for h in range(N_HEADS)], axis=1)
    qt_ref[0] = q.T.astype(BF16)

    kvn = _rms(p2[:, MLA_Q_RANK:MLA_Q_RANK + MLA_KV_RANK], MLA_KV_RANK) * kvn_ref[...]
    kr = _rope(p2[:, MLA_Q_RANK + MLA_KV_RANK:], ta, tb, tc)
    kv = _dot(jnp.concatenate([kvn, kr], axis=1).astype(BF16), w_kv_ref[...])
    k_ref[0] = kv[:, 0:N_HEADS * LANES].astype(BF16)
    vt_ref[0, 0] = kv[:, N_HEADS * LANES:].T.astype(BF16)


def _pre_odd_kernel(h_ref, mod_ref, g_ref, w_in_ref, w_out_ref, w_mix_ref, qg_ref, kg_ref,
                    ta_ref, tb_ref, tc_ref,
                    h1_ref, qt_ref, k_ref, vt_ref, u_ref, act_ref):
    a1 = _pre_common(h_ref, mod_ref, g_ref, w_in_ref, w_out_ref, h1_ref, act_ref)
    ta, tb, tc = ta_ref[...], tb_ref[...], tc_ref[...]

    p = _dot(a1, w_mix_ref[:, 0:O_QKV_COLS])
    scale = GQA_HEAD_DIM ** -0.5 * LOG2_E

    def head(slot, gain):
        x = p[:, slot * LANES:(slot + 1) * LANES]
        return _rope(_rms(x, GQA_HEAD_DIM) * gain, ta, tb, tc)

    q = jnp.concatenate([head(h, qg_ref[...]) * scale for h in range(N_HEADS)], axis=1)
    qt_ref[0] = q.T.astype(BF16)
    k = jnp.concatenate([head(N_HEADS + g, kg_ref[...]) for g in range(GQA_KV_HEADS)], axis=1)
    k_ref[0] = k.astype(BF16)
    vt_ref[0, 0] = p[:, (N_HEADS + GQA_KV_HEADS) * LANES:O_QKV_COLS].T.astype(BF16)

    p2 = _dot(a1, w_mix_ref[:, O_QKV_COLS:O_MIX_COLS])
    u_ref[0] = (p2[:, 0:CONF_WIDTH] * jax.nn.sigmoid(p2[:, CONF_WIDTH:])).astype(BF16)


def _pre_call(kind, h, mod, mod_row, g, w_in, w_out, mix_weights, tables, tile):
    b, n, _ = h.shape
    nt = n // tile
    n_kv = N_HEADS if kind == "even" else GQA_KV_HEADS
    tok = lambda width: pl.BlockSpec((1, tile, width), lambda bi, i: (bi, i, 0))
    in_specs = [
        tok(D_MODEL),
        pl.BlockSpec((1, N_MOD, D_MODEL), lambda bi, i: (mod_row(bi), 0, 0)),
        _resident(g.shape), _resident(w_in.shape), _resident(w_out.shape),
    ] + [_resident(w.shape) for w in mix_weights] + [
        pl.BlockSpec((tile, LANES), lambda bi, i: (i, 0)) for _ in tables
    ]
    out_shape = [
        jax.ShapeDtypeStruct((b, n, D_MODEL), F32),
        jax.ShapeDtypeStruct((b, N_HEADS * LANES, n), BF16),
        jax.ShapeDtypeStruct((b, n, n_kv * LANES), BF16),
        jax.ShapeDtypeStruct((b, nt, n_kv * HEAD_V, tile), BF16),
    ]
    out_specs = [
        tok(D_MODEL),
        pl.BlockSpec((1, N_HEADS * LANES, tile), lambda bi, i: (bi, 0, i)),
        tok(n_kv * LANES),
        pl.BlockSpec((1, 1, n_kv * HEAD_V, tile), lambda bi, i: (bi, i, 0, 0)),
    ]
    n_conv = 2 if kind == "even" else 1
    out_shape += [jax.ShapeDtypeStruct((b, n, SC_WIDTH), BF16)] * n_conv
    out_specs += [tok(SC_WIDTH)] * n_conv
    return pl.pallas_call(
        _pre_even_kernel if kind == "even" else _pre_odd_kernel,
        grid=(b, nt),
        in_specs=in_specs,
        out_specs=out_specs,
        out_shape=out_shape,
        scratch_shapes=[pltpu.VMEM((tile, D_FF), BF16)],
        compiler_params=_params(2),
        name=f"pre_{kind}_{n}",
    )(h, mod, g, w_in, w_out, *mix_weights, *tables)


def _attn_kernel(*refs, n_kv, heads_per_step, sources):
    qt_ref = refs[0]
    kv_refs = refs[1:1 + 2 * len(sources)]
    o_ref, acc_ref, m_ref, l_ref, s_ref = refs[1 + 2 * len(sources):]
    group = N_HEADS // n_kv
    q_tile = qt_ref.shape[2]

    acc_ref[...] = jnp.zeros(acc_ref.shape, F32)
    l_ref[...] = jnp.zeros(l_ref.shape, F32)
    m_ref[...] = jnp.full(m_ref.shape, -jnp.inf, F32)

    def step(heads, k_ref, vt_ref, j, tile):
        m_blk = []
        for u, h in enumerate(heads):
            g = h // group
            kt = k_ref[0, pl.ds(j * tile, tile), g * LANES:(g + 1) * LANES]
            s = _dot(kt, qt_ref[0, h * LANES:(h + 1) * LANES, :])
            s_ref[u, 0:tile, :] = s
            m_blk.append(s.reshape(tile // 8, 8, q_tile).max(axis=0).max(axis=0, keepdims=True))
        for u, h in enumerate(heads):
            g = h // group
            vt = vt_ref[0, j, g * HEAD_V:(g + 1) * HEAD_V, :]
            rows = slice(h * HEAD_V, (h + 1) * HEAD_V)
            m_old = m_ref[h:h + 1, :]
            m_new = jnp.maximum(m_old, m_blk[u])
            alpha = jnp.exp2(m_old - m_new)
            p = jnp.exp2(s_ref[u, 0:tile, :] - m_new)
            p_sum = p.reshape(tile // 8, 8, q_tile).sum(axis=0).sum(axis=0, keepdims=True)
            l_ref[h:h + 1, :] = alpha * l_ref[h:h + 1, :] + p_sum
            acc_ref[rows, :] = alpha * acc_ref[rows, :] + _dot(vt, p.astype(BF16))
            m_ref[h:h + 1, :] = m_new

    for h0 in range(0, N_HEADS, heads_per_step):
        heads = tuple(range(h0, h0 + heads_per_step))
        for si, (n_tiles, tile) in enumerate(sources):
            k_ref, vt_ref = kv_refs[2 * si], kv_refs[2 * si + 1]
            if n_tiles == 1:
                step(heads, k_ref, vt_ref, 0, tile)
            else:
                def body(j, carry, k_ref=k_ref, vt_ref=vt_ref, tile=tile, heads=heads):
                    step(heads, k_ref, vt_ref, j, tile)
                    return carry
                lax.fori_loop(0, n_tiles, body, 0)

    for h in range(N_HEADS):
        rows = slice(h * HEAD_V, (h + 1) * HEAD_V)
        acc_ref[rows, :] = acc_ref[rows, :] / l_ref[h:h + 1, :]
    o_ref[0] = acc_ref[...].T.astype(BF16)


def _attention(qt, kv_sources, n_kv, q_tile, heads_per_step, name):
    b, _, nq = qt.shape
    in_specs = [pl.BlockSpec((1, N_HEADS * LANES, q_tile), lambda bi, i: (bi, 0, i))]
    args = [qt]
    sources = []
    for k, vt in kv_sources:
        s_len, n_tiles, tile = k.shape[1], vt.shape[1], vt.shape[3]
        in_specs.append(pl.BlockSpec((1, s_len, n_kv * LANES), lambda bi, i: (bi, 0, 0),
                                     pipeline_mode=pl.Buffered(1)))
        in_specs.append(pl.BlockSpec((1, n_tiles, n_kv * HEAD_V, tile), lambda bi, i: (bi, 0, 0, 0),
                                     pipeline_mode=pl.Buffered(1)))
        args += [k, vt]
        sources.append((n_tiles, tile))
    kernel = functools.partial(_attn_kernel, n_kv=n_kv, heads_per_step=heads_per_step,
                               sources=tuple(sources))
    max_tile = max(tile for _, tile in sources)
    return pl.pallas_call(
        kernel,
        grid=(b, nq // q_tile),
        in_specs=in_specs,
        out_specs=pl.BlockSpec((1, q_tile, N_HEADS * HEAD_V), lambda bi, i: (bi, i, 0)),
        out_shape=jax.ShapeDtypeStruct((b, nq, N_HEADS * HEAD_V), BF16),
        scratch_shapes=[pltpu.VMEM((N_HEADS * HEAD_V, q_tile), F32),
                        pltpu.VMEM((N_HEADS, q_tile), F32),
                        pltpu.VMEM((N_HEADS, q_tile), F32),
                        pltpu.VMEM((heads_per_step, max_tile, q_tile), F32)],
        compiler_params=_params(2),
        name=name,
    )(*args)


def _conv(u_ref, up_ref, un_ref, cw_ref, ext_ref, taps):
    i, nt = pl.program_id(1), pl.num_programs(1)
    tile = u_ref.shape[1]
    prev = up_ref[0].astype(F32)
    nxt = un_ref[0].astype(F32)
    ext_ref[0:HALO, :] = jnp.where(i > 0, prev, jnp.zeros_like(prev))
    ext_ref[HALO:HALO + tile, :] = u_ref[0].astype(F32)
    ext_ref[HALO + tile:, :] = jnp.where(i < nt - 1, nxt, jnp.zeros_like(nxt))
    acc = None
    for t in range(taps):
        start = HALO - taps // 2 + t
        term = cw_ref[t:t + 1, :] * ext_ref[start:start + tile, :]
        acc = term if acc is None else acc + term
    return acc


def _post_tail(h1_ref, y, mod_ref, g_ref, w_in_ref, w_out_ref, fg_ref, out_ref, act_ref, final):
    m = mod_ref[0]
    h2 = h1_ref[0] + m[5:6] * y
    a2 = _adaln(h2, g_ref[2:3], m[7:8], m[6:7]).astype(BF16)
    h3 = h2 + (0.5 * m[8:9]) * _swiglu(a2, w_in_ref, w_out_ref, act_ref)
    if final:
        h3 = _rms(h3, D_MODEL) * fg_ref[...]
    out_ref[0] = h3


def _post_even_kernel(h1_ref, o_ref, sb_ref, u_ref, up_ref, un_ref, mod_ref, g_ref, cw_ref,
                      w_mo_ref, w_in_ref, w_out_ref, fg_ref, out_ref, act_ref, ext_ref, *, final):
    conv = _conv(u_ref, up_ref, un_ref, cw_ref, ext_ref, SC_KERNEL)
    ya = (sb_ref[0].astype(F32) * conv).astype(BF16)
    y = _dot(ya, w_mo_ref[0:SC_WIDTH, :]) + _dot(o_ref[0], w_mo_ref[SC_WIDTH:, :])
    _post_tail(h1_ref, y, mod_ref, g_ref, w_in_ref, w_out_ref, fg_ref, out_ref, act_ref, final)


def _post_odd_kernel(h1_ref, o_ref, u_ref, up_ref, un_ref, mod_ref, g_ref, cw_ref, cb_ref, lg_ref, lb_ref,
                     w_mo_ref, w_in_ref, w_out_ref, fg_ref, out_ref, act_ref, ext_ref, *, final):
    z = _conv(u_ref, up_ref, un_ref, cw_ref, ext_ref, CONF_KERNEL) + cb_ref[...]
    mu = jnp.mean(z, axis=-1, keepdims=True)
    zc = z - mu
    var = jnp.mean(zc * zc, axis=-1, keepdims=True)
    zn = zc * lax.rsqrt(var + NORM_EPS) * lg_ref[...] + lb_ref[...]
    cf = (zn * jax.nn.sigmoid(zn)).astype(BF16)
    attn_cols = N_HEADS * HEAD_V
    y = _dot(o_ref[0], w_mo_ref[0:attn_cols, :]) + _dot(cf, w_mo_ref[attn_cols:, :])
    _post_tail(h1_ref, y, mod_ref, g_ref, w_in_ref, w_out_ref, fg_ref, out_ref, act_ref, final)


def _post_call(kind, h1, o, conv_inputs, mod, mod_row, g, conv_params, w_mo, w_in, w_out, final_g, tile, final):
    b, n, _ = h1.shape
    nt = n // tile
    halo_per_tile = tile // HALO
    n_halo = n // HALO
    tok = lambda width: pl.BlockSpec((1, tile, width), lambda bi, i: (bi, i, 0))
    u = conv_inputs[-1]
    in_specs = [tok(D_MODEL), tok(N_HEADS * HEAD_V)] + [tok(SC_WIDTH) for _ in conv_inputs] + [
        pl.BlockSpec((1, HALO, SC_WIDTH), lambda bi, i: (bi, jnp.maximum(i * halo_per_tile - 1, 0), 0)),
        pl.BlockSpec((1, HALO, SC_WIDTH), lambda bi, i: (bi, jnp.minimum((i + 1) * halo_per_tile, n_halo - 1), 0)),
        pl.BlockSpec((1, N_MOD, D_MODEL), lambda bi, i: (mod_row(bi), 0, 0)),
        _resident(g.shape),
    ] + [_resident(c.shape) for c in conv_params] + [
        _resident(w_mo.shape), _resident(w_in.shape), _resident(w_out.shape), _resident(final_g.shape),
    ]
    kernel = functools.partial(_post_even_kernel if kind == "even" else _post_odd_kernel, final=final)
    return pl.pallas_call(
        kernel,
        grid=(b, nt),
        in_specs=in_specs,
        out_specs=tok(D_MODEL),
        out_shape=jax.ShapeDtypeStruct((b, n, D_MODEL), F32),
        scratch_shapes=[pltpu.VMEM((tile, D_FF), BF16),
                        pltpu.VMEM((tile + 2 * HALO, SC_WIDTH), F32)],
        compiler_params=_params(2),
        name=f"post_{kind}_{n}",
    )(h1, o, *conv_inputs, u, u, mod, g, *conv_params, w_mo, w_in, w_out, final_g)


def _rope_tables(n_rows, rot_dim, lane_offset):
    row = jnp.repeat(jnp.arange(n_rows, dtype=F32), GRID_W)
    col = jnp.tile(jnp.arange(GRID_W, dtype=F32), n_rows)
    nf = rot_dim // 4
    inv = ROPE_THETA ** (-jnp.arange(nf, dtype=F32) / nf)
    ang = jnp.concatenate([row[:, None] * inv, col[:, None] * inv], axis=-1)
    cos = jnp.repeat(jnp.cos(ang), 2, axis=-1)
    sin = jnp.repeat(jnp.sin(ang), 2, axis=-1)
    even = (jnp.arange(rot_dim) % 2 == 0)
    n = cos.shape[0]
    pad = lambda t, fill: jnp.concatenate(
        [jnp.full((n, lane_offset), fill, F32), t, jnp.full((n, LANES - lane_offset - rot_dim), fill, F32)], axis=1)
    return (pad(cos, 1.0), pad(jnp.where(even, -sin, 0.0), 0.0), pad(jnp.where(even, 0.0, sin), 0.0))


def _identity_tables(n):
    return (jnp.ones((n, LANES), F32), jnp.zeros((n, LANES), F32), jnp.zeros((n, LANES), F32))


def _pad_heads(w, n_heads, width):
    k = w.shape[0]
    w = w.reshape(k, n_heads, width)
    return jnp.pad(w, ((0, 0), (0, 0), (0, LANES - width))).reshape(k, n_heads * LANES)


def _even_weights(w_in, q_norm, w_q_b, kv_norm, w_kv_b):
    cut = 3 * SC_WIDTH + MLA_Q_RANK + MLA_KV_RANK
    kr_slot = jnp.pad(w_in[:, cut:], ((0, 0), (MLA_NOPE, LANES - MLA_NOPE - MLA_ROPE)))
    w_mix = jnp.concatenate([w_in[:, :cut], kr_slot], axis=1).astype(BF16)
    w_q = _pad_heads(w_q_b, N_HEADS, MLA_NOPE + MLA_ROPE).astype(BF16)
    kvb = w_kv_b.reshape(MLA_KV_RANK, N_HEADS, MLA_NOPE + HEAD_V)
    w_k = _pad_heads(kvb[:, :, :MLA_NOPE].reshape(MLA_KV_RANK, -1), N_HEADS, MLA_NOPE)
    w_v = kvb[:, :, MLA_NOPE:].reshape(MLA_KV_RANK, N_HEADS * HEAD_V)
    place = jnp.zeros((LANES, N_HEADS, LANES), F32)
    idx = jnp.arange(MLA_NOPE, MLA_NOPE + MLA_ROPE)
    place = place.at[idx, :, idx].set(1.0).reshape(LANES, N_HEADS * LANES)
    w_kv = jnp.concatenate([
        jnp.concatenate([w_k, w_v], axis=1),
        jnp.concatenate([place, jnp.zeros((LANES, N_HEADS * HEAD_V), F32)], axis=1)], axis=0).astype(BF16)
    return [w_mix, q_norm.reshape(1, -1), kv_norm.reshape(1, -1), w_q, w_kv]


def _odd_weights(w_in, q_norm, k_norm):
    qw = N_HEADS * GQA_HEAD_DIM
    kw = GQA_KV_HEADS * GQA_HEAD_DIM
    w_mix = jnp.concatenate([
        _pad_heads(w_in[:, :qw], N_HEADS, GQA_HEAD_DIM),
        _pad_heads(w_in[:, qw:qw + kw], GQA_KV_HEADS, GQA_HEAD_DIM),
        w_in[:, qw + kw:]], axis=1).astype(BF16)
    pad = lambda g: jnp.pad(g, (0, LANES - GQA_HEAD_DIM)).reshape(1, LANES)
    return [w_mix, pad(q_norm), pad(k_norm)]


def kernel(x, c, ctx, c_ctx, w_mod, b_mod, norm_g, w_ffn_in, w_ffn_out, e_w_in, e_conv_w, e_q_norm, e_w_q_b, e_kv_norm, e_w_kv_b, e_w_out, o_w_in, o_q_norm, o_k_norm, o_conv_w, o_conv_b, o_ln_g, o_ln_b, o_w_out, final_g):
    batch, seq, _ = x.shape
    ctx_len = ctx.shape[1]
    assert seq % TOKEN_TILE == 0 and seq % Q_TILE == 0 and ctx_len % HALO == 0 and ctx_len % LANES == 0

    cond = jnp.concatenate([c, c_ctx[None, :]], axis=0)
    cond_t = jnp.broadcast_to(cond[:, :, None], (batch + 1, D_MODEL, LANES))
    mod_all = _modulation(cond_t, w_mod, b_mod)[:, :batch + 1].reshape(DEPTH, batch + 1, N_MOD, D_MODEL)

    w_ffn_in = w_ffn_in.astype(BF16)
    w_ffn_out = w_ffn_out.astype(BF16)
    final_g2 = final_g.reshape(1, D_MODEL)
    lat_row = lambda bi: bi
    ctx_row = lambda bi: batch

    h, hc = x, ctx
    for l in range(DEPTH):
        last = l == DEPTH - 1
        i = l // 2
        mod = mod_all[l]
        g = norm_g[l]
        if l % 2 == 0:
            kind, n_kv, heads_per_step = "even", N_HEADS, 4
            mix_w = _even_weights(e_w_in[i], e_q_norm[i], e_w_q_b[i], e_kv_norm[i], e_w_kv_b[i])
            tables = _rope_tables(seq // GRID_W, MLA_ROPE, MLA_NOPE)
            conv_params = [e_conv_w[i]]
            w_mo = e_w_out[i].astype(BF16)
        else:
            kind, n_kv, heads_per_step = "odd", GQA_KV_HEADS, N_HEADS // GQA_KV_HEADS
            mix_w = _odd_weights(o_w_in[i], o_q_norm[i], o_k_norm[i])
            tables = _rope_tables(seq // GRID_W, GQA_HEAD_DIM, 0)
            conv_params = [o_conv_w[i], o_conv_b[i].reshape(1, -1), o_ln_g[i].reshape(1, -1), o_ln_b[i].reshape(1, -1)]
            w_mo = o_w_out[i].astype(BF16)

        lat = _pre_call(kind, h, mod, lat_row, g, w_ffn_in[l, 0], w_ffn_out[l, 0], mix_w, tables, TOKEN_TILE)
        cx = _pre_call(kind, hc, mod, ctx_row, g, w_ffn_in[l, 0], w_ffn_out[l, 0], mix_w,
                       _identity_tables(ctx_len), ctx_len)
        h1, qt, k, vt = lat[:4]
        hc1, qtc, kc, vtc = cx[:4]

        o = _attention(qt, [(k, vt), (kc, vtc)], n_kv, Q_TILE, heads_per_step, f"attn_{kind}")
        h = _post_call(kind, h1, o, lat[4:], mod, lat_row, g, conv_params, w_mo,
                       w_ffn_in[l, 1], w_ffn_out[l, 1], final_g2, TOKEN_TILE, final=last)
        if not last:
            oc = _attention(qtc, [(kc, vtc)], n_kv, ctx_len, heads_per_step, f"attn_{kind}_ctx")
            hc = _post_call(kind, hc1, oc, cx[4:], mod, ctx_row, g, conv_params, w_mo,
                            w_ffn_in[l, 1], w_ffn_out[l, 1], final_g2, ctx_len, final=False)
    return h
```
